```python
import math
import jax, jax.numpy as jnp
from jax import lax
import numpy as np

D_MODEL = 4096
BATCH = 4
SEQ = 2048
DEPTH = 1

A_HEADS = 16
A_HEAD_DIM = 128
IDX_HEADS = 16
IDX_DIM = 64
IDX_TOPK_MAX = 256
A_Q_BLOCK = 32
REL_BUCKETS = 32
REL_MAX_DIST = 128
B_HEADS = 16
B_HEAD_DIM = 128
CONV_WIDTH = 4
DELTA_CHUNK = 64
PEER_HEADS = 8
PEER_QUERY_DIM = 256
PEER_N_KEYS = 128
PEER_TOPK = 16
PEER_N_EXPERTS = PEER_N_KEYS * PEER_N_KEYS
PEER_TOKEN_BLOCK = 32

RMS_EPS = 1e-6
A_WIDTH = A_HEADS * A_HEAD_DIM
B_WIDTH = B_HEADS * B_HEAD_DIM
SPLIT_SIZES = (A_WIDTH, A_WIDTH, A_WIDTH,
               IDX_HEADS * IDX_DIM, IDX_DIM, IDX_HEADS,
               3 * B_WIDTH, B_HEADS, B_HEADS, B_WIDTH,
               D_MODEL, D_MODEL)

kernel_name = 'hybrid_dsa_gdn_peer_block'


def rms_norm(x, g):
    xf = x.astype(jnp.float32)
    y = xf * lax.rsqrt(jnp.mean(xf * xf, axis=-1, keepdims=True) + RMS_EPS)
    return (y * g.astype(jnp.float32)).astype(x.dtype)


def l2_norm(x):
    xf = x.astype(jnp.float32)
    return xf * lax.rsqrt(jnp.sum(xf * xf, axis=-1, keepdims=True) + RMS_EPS)


def split_columns(y):
    parts = []
    start = 0
    for size in SPLIT_SIZES:
        parts.append(y[..., start:start + size])
        start += size
    return parts


def t5_bucket(dist):
    n = jnp.maximum(dist, 0)
    max_exact = REL_BUCKETS // 2
    nf = jnp.maximum(n, 1).astype(jnp.float32)
    large = max_exact + (jnp.log(nf / max_exact) / math.log(REL_MAX_DIST / max_exact)
                         * (REL_BUCKETS - max_exact)).astype(jnp.int32)
    large = jnp.minimum(large, REL_BUCKETS - 1)
    return jnp.where(n < max_exact, n, large)


def causal_depthwise_conv(x, w):
    c = x.shape[-1]
    return lax.conv_general_dilated(
        x, w[:, None, :].astype(x.dtype), window_strides=(1,),
        padding=[(CONV_WIDTH - 1, 0)], dimension_numbers=('NWC', 'WIO', 'NWC'),
        feature_group_count=c)


def dsa_attention(q, k, v, qi, ki, wi, rel_bias):
    bsz, s = q.shape[:2]
    topk = min(IDX_TOPK_MAX, s // 4)
    nblk = s // A_Q_BLOCK
    pos = jnp.arange(s, dtype=jnp.int32)
    idx_scale = (IDX_DIM ** -0.5) * (IDX_HEADS ** -0.5)
    attn_scale = A_HEAD_DIM ** -0.5

    def to_blocks(t):
        return jnp.moveaxis(t.reshape((bsz, nblk, A_Q_BLOCK) + t.shape[2:]), 1, 0)

    def block(args):
        qb, qib, wib, tb = args
        s_idx = jnp.einsum('bqhd,bsd->bqhs', qib, ki).astype(jnp.float32)
        score = jnp.einsum('bqh,bqhs->bqs', wib.astype(jnp.float32), jax.nn.relu(s_idx)) * idx_scale
        causal = pos[None, None, :] <= tb[None, :, None]
        score = jnp.where(causal, score, -jnp.inf)
        _, sel = lax.top_k(score, topk)
        valid = sel <= tb[None, :, None]
        kg = jax.vmap(lambda kk, ii: kk[ii])(k, sel)
        vg = jax.vmap(lambda vv, ii: vv[ii])(v, sel)
        logits = jnp.einsum('bqhd,bqkhd->bqhk', qb, kg).astype(jnp.float32) * attn_scale
        bias = rel_bias[t5_bucket(tb[None, :, None] - sel)]
        logits = logits + jnp.moveaxis(bias.astype(jnp.float32), -1, -2)
        logits = jnp.where(valid[:, :, None, :], logits, -jnp.inf)
        p = jax.nn.softmax(logits, axis=-1).astype(v.dtype)
        return jnp.einsum('bqhk,bqkhd->bqhd', p, vg)

    out = lax.map(block, (to_blocks(q), to_blocks(qi), to_blocks(wi), pos.reshape(nblk, A_Q_BLOCK)))
    return jnp.moveaxis(out, 0, 1).reshape(bsz, s, A_WIDTH)


def gated_delta_rule(q, k, v, g, beta):
    bsz, s, h, dk = q.shape
    dv = v.shape[-1]
    c = DELTA_CHUNK
    nc = s // c

    def chunks(t):
        t = jnp.moveaxis(t.astype(jnp.float32), 2, 1)
        return t.reshape((bsz, h, nc, c) + t.shape[3:])

    q, k, v, g, beta = [chunks(t) for t in (q, k, v, g, beta)]
    q = q * (dk ** -0.5)
    g = jnp.cumsum(g, axis=-1)
    k_beta = k * beta[..., None]
    v_beta = v * beta[..., None]
    lower = jnp.tril(jnp.ones((c, c), dtype=bool))
    strict = jnp.tril(jnp.ones((c, c), dtype=bool), -1)
    diff = g[..., :, None] - g[..., None, :]
    decay = jnp.where(lower, jnp.exp(jnp.where(lower, diff, 0.0)), 0.0)
    eye = jnp.eye(c, dtype=jnp.float32)
    kk = jnp.einsum('bhncd,bhnsd->bhncs', k_beta, k) * decay
    a_mat = eye + jnp.where(strict, kk, 0.0)
    t_mat = lax.linalg.triangular_solve(a_mat, jnp.broadcast_to(eye, a_mat.shape),
                                        left_side=True, lower=True, unit_diagonal=True)
    u = t_mat @ v_beta
    w = t_mat @ (k_beta * jnp.exp(g)[..., None])
    qk = jnp.einsum('bhncd,bhnsd->bhncs', q, k) * decay
    g_last = g[..., -1:]
    k_dec = k * jnp.exp(g_last - g)[..., None]
    q_dec = q * jnp.exp(g)[..., None]

    def step(state, xs):
        u_i, w_i, qk_i, qd_i, kd_i, gl_i = xs
        v_new = u_i - w_i @ state
        o = qd_i @ state + qk_i @ v_new
        state = state * jnp.exp(gl_i)[..., None] + jnp.swapaxes(kd_i, -1, -2) @ v_new
        return state, o

    xs = [jnp.moveaxis(t, 2, 0) for t in (u, w, qk, q_dec, k_dec, g_last)]
    state0 = jnp.zeros((bsz, h, dk, dv), jnp.float32)
    _, o = lax.scan(step, state0, tuple(xs))
    o = jnp.moveaxis(o, 0, 2).reshape(bsz, h, s, dv)
    return jnp.moveaxis(o, 1, 2)


def peer_ffn(xn, wq, k1, k2, u_tab, v_tab):
    bsz, s, d = xn.shape
    xt = xn.reshape(-1, d)
    t = xt.shape[0]
    half = PEER_QUERY_DIM // 2
    q = (xt @ wq).reshape(t, PEER_HEADS, PEER_QUERY_DIM)
    s1 = jnp.einsum('thd,kd->thk', q[..., :half], k1).astype(jnp.float32)
    s2 = jnp.einsum('thd,kd->thk', q[..., half:], k2).astype(jnp.float32)
    v1, i1 = lax.top_k(s1, PEER_TOPK)
    v2, i2 = lax.top_k(s2, PEER_TOPK)
    cand = (v1[..., :, None] + v2[..., None, :]).reshape(t, PEER_HEADS, PEER_TOPK * PEER_TOPK)
    cand_idx = (i1[..., :, None] * PEER_N_KEYS + i2[..., None, :]).reshape(t, PEER_HEADS, PEER_TOPK * PEER_TOPK)
    sc, ci = lax.top_k(cand, PEER_TOPK)
    experts = jnp.take_along_axis(cand_idx, ci, axis=-1)
    gates = jax.nn.softmax(sc, axis=-1).astype(xn.dtype)
    nblk = t // PEER_TOKEN_BLOCK

    def block(args):
        xb, eb, gb = args
        ug = u_tab[eb]
        act = jax.nn.gelu(jnp.einsum('td,thkd->thk', xb, ug), approximate=False)
        vg = v_tab[eb]
        return jnp.einsum('thk,thkd->td', gb * act, vg)

    out = lax.map(block, (xt.reshape(nblk, PEER_TOKEN_BLOCK, d),
                          experts.reshape(nblk, PEER_TOKEN_BLOCK, PEER_HEADS, PEER_TOPK),
                          gates.reshape(nblk, PEER_TOKEN_BLOCK, PEER_HEADS, PEER_TOPK)))
    return out.reshape(bsz, s, d)


def hybrid_layer(x, norm1_g, w_in, conv_w, a_log, dt_bias, gdn_norm_g, q_norm_g, k_norm_g,
                 rel_bias, w_br_a, w_br_b, w_out, norm2_g, peer_wq, peer_k1, peer_k2, peer_u, peer_v):
    bsz, s, _ = x.shape
    xn = rms_norm(x, norm1_g)
    proj = xn @ w_in
    (aq, ak, av, iq, ik, iw, bqkv, ba, bb, bz, gate_a, gate_b) = split_columns(proj)

    aq = rms_norm(aq.reshape(bsz, s, A_HEADS, A_HEAD_DIM), q_norm_g)
    ak = rms_norm(ak.reshape(bsz, s, A_HEADS, A_HEAD_DIM), k_norm_g)
    av = av.reshape(bsz, s, A_HEADS, A_HEAD_DIM)
    iq = iq.reshape(bsz, s, IDX_HEADS, IDX_DIM)
    y_a = dsa_attention(aq, ak, av, iq, ik, iw, rel_bias)

    bqkv = jax.nn.silu(causal_depthwise_conv(bqkv, conv_w))
    bq, bk, bv = bqkv[..., :B_WIDTH], bqkv[..., B_WIDTH:2 * B_WIDTH], bqkv[..., 2 * B_WIDTH:]
    bq = l2_norm(bq.reshape(bsz, s, B_HEADS, B_HEAD_DIM))
    bk = l2_norm(bk.reshape(bsz, s, B_HEADS, B_HEAD_DIM))
    bv = bv.reshape(bsz, s, B_HEADS, B_HEAD_DIM)
    g = -jnp.exp(a_log.astype(jnp.float32)) * jax.nn.softplus(ba.astype(jnp.float32) + dt_bias.astype(jnp.float32))
    beta = jax.nn.sigmoid(bb.astype(jnp.float32))
    o = gated_delta_rule(bq, bk, bv, g, beta)
    z = bz.reshape(bsz, s, B_HEADS, B_HEAD_DIM).astype(jnp.float32)
    y_b = (rms_norm(o, gdn_norm_g) * jax.nn.silu(z)).reshape(bsz, s, B_WIDTH).astype(x.dtype)

    merged = jax.nn.sigmoid(gate_a) * (y_a @ w_br_a) + jax.nn.sigmoid(gate_b) * (y_b @ w_br_b)
    x = x + merged @ w_out

    x = x + peer_ffn(rms_norm(x, norm2_g), peer_wq, peer_k1, peer_k2, peer_u, peer_v)
    return x


def setup_inputs(seed: int = 0) -> dict:
    key = jax.random.key(seed)
    ks = jax.random.split(key, 20)
    f32 = jnp.float32
    in_cols = sum(SPLIT_SIZES)

    def nrm(k, shape, scale):
        return jax.random.normal(k, shape, f32) * scale

    dt = jnp.exp(jax.random.uniform(ks[5], (DEPTH, B_HEADS), f32, math.log(1e-3), math.log(1e-1)))
    return {
        'x': nrm(ks[0], (BATCH, SEQ, D_MODEL), 1.0),
        'norm1_g': 1.0 + nrm(ks[1], (DEPTH, D_MODEL), 0.05),
        'w_in': nrm(ks[2], (DEPTH, D_MODEL, in_cols), D_MODEL ** -0.5),
        'conv_w': nrm(ks[3], (DEPTH, CONV_WIDTH, 3 * B_WIDTH), CONV_WIDTH ** -0.5),
        'a_log': jnp.log(jax.random.uniform(ks[4], (DEPTH, B_HEADS), f32, 1.0, 16.0)),
        'dt_bias': dt + jnp.log(-jnp.expm1(-dt)),
        'gdn_norm_g': 1.0 + nrm(ks[6], (DEPTH, B_HEAD_DIM), 0.05),
        'q_norm_g': 1.0 + nrm(ks[7], (DEPTH, A_HEAD_DIM), 0.05),
        'k_norm_g': 1.0 + nrm(ks[8], (DEPTH, A_HEAD_DIM), 0.05),
        'rel_bias': nrm(ks[9], (REL_BUCKETS, A_HEADS), 0.5),
        'w_br_a': nrm(ks[10], (DEPTH, A_WIDTH, D_MODEL), A_WIDTH ** -0.5),
        'w_br_b': nrm(ks[11], (DEPTH, B_WIDTH, D_MODEL), B_WIDTH ** -0.5),
        'w_out': nrm(ks[12], (DEPTH, D_MODEL, D_MODEL), D_MODEL ** -0.5),
        'norm2_g': 1.0 + nrm(ks[13], (DEPTH, D_MODEL), 0.05),
        'peer_wq': nrm(ks[14], (DEPTH, D_MODEL, PEER_HEADS * PEER_QUERY_DIM), D_MODEL ** -0.5),
        'peer_k1': nrm(ks[15], (DEPTH, PEER_N_KEYS, PEER_QUERY_DIM // 2), (PEER_QUERY_DIM // 2) ** -0.5),
        'peer_k2': nrm(ks[16], (DEPTH, PEER_N_KEYS, PEER_QUERY_DIM // 2), (PEER_QUERY_DIM // 2) ** -0.5),
        'peer_u': nrm(ks[17], (DEPTH, PEER_N_EXPERTS, D_MODEL), D_MODEL ** -0.5),
        'peer_v': nrm(ks[18], (DEPTH, PEER_N_EXPERTS, D_MODEL), PEER_HEADS ** -0.5),
    }


def reference(x, norm1_g, w_in, conv_w, a_log, dt_bias, gdn_norm_g, q_norm_g, k_norm_g, rel_bias,
              w_br_a, w_br_b, w_out, norm2_g, peer_wq, peer_k1, peer_k2, peer_u, peer_v):
    for l in range(DEPTH):
        x = hybrid_layer(x, norm1_g[l], w_in[l], conv_w[l], a_log[l], dt_bias[l], gdn_norm_g[l],
                         q_norm_g[l], k_norm_g[l], rel_bias, w_br_a[l], w_br_b[l], w_out[l],
                         norm2_g[l], peer_wq[l], peer_k1[l], peer_k2[l], peer_u[l], peer_v[l])
    return x
```

```python
import functools
import math

import jax
import jax.numpy as jnp
from jax import lax
from jax.experimental import pallas as pl
from jax.experimental.pallas import tpu as pltpu

F32 = jnp.float32
BF16 = jnp.bfloat16

A_HEADS = 16
A_HEAD_DIM = 128
IDX_HEADS = 16
IDX_DIM = 64
IDX_TOPK_MAX = 256
REL_BUCKETS = 32
REL_MAX_DIST = 128
B_HEADS = 16
B_HEAD_DIM = 128
CONV_WIDTH = 4
DELTA_CHUNK = 64
PEER_HEADS = 8
PEER_QUERY_DIM = 256
PEER_N_KEYS = 128
PEER_TOPK = 16
RMS_EPS = 1e-6
A_WIDTH = A_HEADS * A_HEAD_DIM
B_WIDTH = B_HEADS * B_HEAD_DIM

LANES = 128
VMEM_LIMIT_BYTES = 56 * 1024 * 1024
NEG_BIG = -1e30
KEY_NEG_INF = -2139095041
GDN_HEAD_GROUP = 4
GDN_SUPER = 256
PEER_ACT_FIRST_KEYS = 8


def _params(*sem):
    return pltpu.CompilerParams(dimension_semantics=sem, vmem_limit_bytes=VMEM_LIMIT_BYTES)


def _pick(n, cands):
    for c in cands:
        if c <= n and n % c == 0:
            return c
    return n


def _dot(a, b, precision=None):
    return jnp.dot(a, b, preferred_element_type=F32, precision=precision)


def _dot_nt(a, b):
    return lax.dot_general(a, b, (((1,), (1,)), ((), ())), preferred_element_type=F32)


def _dot_tn(a, b):
    return lax.dot_general(a, b, (((0,), (0,)), ((), ())), preferred_element_type=F32)


def _rmsnorm_body(x_ref, g_ref, o_ref):
    x = x_ref[...].astype(F32)
    ms = jnp.mean(x * x, axis=-1, keepdims=True)
    o_ref[...] = (x * lax.rsqrt(ms + RMS_EPS) * g_ref[...]).astype(o_ref.dtype)


def _rmsnorm(x2d, g, out_dtype=BF16):
    t, d = x2d.shape
    tm = _pick(t, (256, 128, 64, 32, 16, 8))
    return pl.pallas_call(
        _rmsnorm_body,
        grid=(t // tm,),
        in_specs=[pl.BlockSpec((tm, d), lambda i: (i, 0)), pl.BlockSpec((1, d), lambda i: (0, 0))],
        out_specs=pl.BlockSpec((tm, d), lambda i: (i, 0)),
        out_shape=jax.ShapeDtypeStruct((t, d), out_dtype),
        compiler_params=_params("parallel"),
        name="rmsnorm",
    )(x2d, g.reshape(1, d).astype(F32))


def _mm_body(*refs, nk, has_res):
    if has_res:
        a_ref, b_ref, r_ref, o_ref = refs[:4]
        rest = refs[4:]
    else:
        a_ref, b_ref, o_ref = refs[:3]
        r_ref = None
        rest = refs[3:]
    part = _dot(a_ref[...], b_ref[...])
    if nk == 1:
        if has_res:
            part = part + r_ref[...]
        o_ref[...] = part.astype(o_ref.dtype)
        return
    (acc_ref,) = rest
    k = pl.program_id(2)

    @pl.when(k == 0)
    def _():
        acc_ref[...] = part + r_ref[...] if has_res else part

    @pl.when(k > 0)
    def _():
        acc_ref[...] += part

    @pl.when(k == nk - 1)
    def _():
        o_ref[...] = acc_ref[...].astype(o_ref.dtype)


def _matmul(a, b, *, out_dtype, bm, bn, bk=None, residual=None, name="matmul"):
    m, kdim = a.shape
    _, n = b.shape
    bk = kdim if bk is None else bk
    nk = kdim // bk
    has_res = residual is not None
    in_specs = [pl.BlockSpec((bm, bk), lambda i, j, k: (i, k)), pl.BlockSpec((bk, bn), lambda i, j, k: (k, j))]
    args = [a, b]
    if has_res:
        in_specs.append(pl.BlockSpec((bm, bn), lambda i, j, k: (i, j)))
        args.append(residual)
    scratch = [pltpu.VMEM((bm, bn), F32)] if nk > 1 else []
    return pl.pallas_call(
        functools.partial(_mm_body, nk=nk, has_res=has_res),
        grid=(m // bm, n // bn, nk),
        in_specs=in_specs,
        out_specs=pl.BlockSpec((bm, bn), lambda i, j, k: (i, j)),
        out_shape=jax.ShapeDtypeStruct((m, n), out_dtype),
        scratch_shapes=scratch,
        compiler_params=_params("parallel", "parallel", "arbitrary"),
        name=name,
    )(*args)


def _qknorm_body(x_ref, g_ref, o_ref):
    for hh in range(2 * A_HEADS):
        sl = slice(hh * A_HEAD_DIM, (hh + 1) * A_HEAD_DIM)
        x = x_ref[:, sl].astype(F32)
        ms = jnp.mean(x * x, axis=-1, keepdims=True)
        o_ref[:, sl] = (x * lax.rsqrt(ms + RMS_EPS) * g_ref[:, sl]).astype(o_ref.dtype)


def _qknorm(proj_big, g_row):
    t = proj_big.shape[0]
    w = 2 * A_WIDTH
    tm = _pick(t, (256, 128, 64, 32, 16, 8))
    return pl.pallas_call(
        _qknorm_body,
        grid=(t // tm,),
        in_specs=[pl.BlockSpec((tm, w), lambda i: (i, 0)), pl.BlockSpec((1, w), lambda i: (0, 0))],
        out_specs=pl.BlockSpec((tm, w), lambda i: (i, 0)),
        out_shape=jax.ShapeDtypeStruct((t, w), BF16),
        compiler_params=_params("parallel"),
        name="qknorm",
    )(proj_big, g_row)


def _bias_body(rb_ref, o_ref, *, tq):
    h = pl.program_id(0)
    r = lax.broadcasted_iota(jnp.int32, (tq, tq), 0)
    c = lax.broadcasted_iota(jnp.int32, (tq, tq), 1)
    max_exact = REL_BUCKETS // 2
    for d in range(3):
        n = jnp.maximum(d * tq + r - c, 0)
        nf = jnp.maximum(n, 1).astype(F32)
        large = max_exact + (jnp.log(nf / max_exact) / math.log(REL_MAX_DIST / max_exact)
                             * (REL_BUCKETS - max_exact)).astype(jnp.int32)
        large = jnp.minimum(large, REL_BUCKETS - 1)
        bucket = jnp.where(n < max_exact, n, large)
        acc = jnp.zeros((tq, tq), F32)
        for j in range(REL_BUCKETS):
            acc = jnp.where(bucket == j, rb_ref[h, j], acc)
        o_ref[d] = acc


def _bias_table(rel_bias, tq):
    return pl.pallas_call(
        functools.partial(_bias_body, tq=tq),
        grid=(A_HEADS,),
        in_specs=[pl.BlockSpec(memory_space=pltpu.SMEM)],
        out_specs=pl.BlockSpec((None, 3, tq, tq), lambda h: (h, 0, 0, 0)),
        out_shape=jax.ShapeDtypeStruct((A_HEADS, 3, tq, tq), F32),
        compiler_params=_params("arbitrary"),
        name="t5_bias_table",
    )(rel_bias.T.astype(F32))


def _dsa_body(q_ref, k_ref, v_ref, iq_ref, ik_ref, iw_ref, bias_ref, o_ref, mask_ref, key_ref, *, tq, nk, topk):
    i = pl.program_id(1)
    h = pl.program_id(2)
    idx_scale = (IDX_DIM ** -0.5) * (IDX_HEADS ** -0.5)

    @pl.when(h == 0)
    def _build_mask():
        iw = iw_ref[...]
        row = i * tq + lax.broadcasted_iota(jnp.int32, (tq, tq), 0)
        col = lax.broadcasted_iota(jnp.int32, (tq, tq), 1)
        iqb = [iq_ref[:, hh * IDX_DIM:(hh + 1) * IDX_DIM].astype(BF16) for hh in range(IDX_HEADS)]
        for j in range(nk):
            @pl.when(j <= i)
            def _():
                ikj = ik_ref[j * tq:(j + 1) * tq, 0:IDX_DIM].astype(BF16)
                acc = jnp.zeros((tq, tq), F32)
                for hh in range(IDX_HEADS):
                    s = _dot_nt(iqb[hh], ikj)
                    acc = acc + iw[:, hh:hh + 1] * jnp.maximum(s, 0.0)
                acc = acc * idx_scale
                sc = jnp.where(j * tq + col <= row, acc, -jnp.inf)
                bits = pltpu.bitcast(sc, jnp.int32)
                key_ref[j] = jnp.where(bits < 0, bits ^ 0x7FFFFFFF, bits)

            @pl.when(j > i)
            def _():
                key_ref[j] = jnp.full((tq, tq), KEY_NEG_INF, jnp.int32)

        def count_ge(cand):
            tot = jnp.zeros((tq, tq), F32)
            for j in range(nk):
                tot = tot + jnp.where(key_ref[j] >= cand, 1.0, 0.0)
            return jnp.sum(tot, axis=1, keepdims=True)

        kf = float(topk)
        c0 = count_ge(jnp.zeros((tq, 1), jnp.int32))
        lo0 = jnp.where(c0 >= kf, 0, -2147483648).astype(jnp.int32)

        def search(b, lo):
            cand = lo + lax.shift_left(jnp.int32(1), 30 - b)
            return jnp.where(count_ge(cand) >= kf, cand, lo)

        lo = lax.fori_loop(0, 31, search, lo0)
        thr = jnp.maximum(lo, KEY_NEG_INF + 1)
        for j in range(nk):
            mask_ref[j] = jnp.where(key_ref[j] >= thr, 0.0, NEG_BIG)

    q = q_ref[...]

    def kv_step(j, carry):
        m, l, acc = carry
        off = pl.multiple_of(j * tq, tq)
        kj = k_ref[pl.ds(off, tq), :]
        vj = v_ref[pl.ds(off, tq), :]
        s = _dot_nt(q, kj) + bias_ref[jnp.minimum(i - j, 2)] + mask_ref[j]
        m_new = jnp.maximum(m, jnp.max(s, axis=1, keepdims=True))
        p = jnp.exp(s - m_new)
        alpha = jnp.exp(m - m_new)
        l = alpha * l + jnp.sum(p, axis=1, keepdims=True)
        acc = alpha * acc + _dot(p.astype(BF16), vj)
        return m_new, l, acc

    m0 = jnp.full((tq, 1), NEG_BIG, F32)
    l0 = jnp.zeros((tq, 1), F32)
    acc0 = jnp.zeros((tq, A_HEAD_DIM), F32)
    _, l, acc = lax.fori_loop(0, i + 1, kv_step, (m0, l0, acc0))
    o_ref[...] = (acc / l).astype(o_ref.dtype)


def _dsa(qkn, proj_big, proj_small, bias_tab, bsz, s, tq, topk):
    nk = s // tq
    nb = proj_big.shape[-1]
    ns = proj_small.shape[-1]
    qkn3 = qkn.reshape(bsz, s, 2 * A_WIDTH)
    pb3 = proj_big.reshape(bsz, s, nb)
    ps3 = proj_small.reshape(bsz, s, ns)
    v_blk = 2 * A_WIDTH // A_HEAD_DIM
    return pl.pallas_call(
        functools.partial(_dsa_body, tq=tq, nk=nk, topk=topk),
        grid=(bsz, nk, A_HEADS),
        in_specs=[
            pl.BlockSpec((None, tq, A_HEAD_DIM), lambda b, i, h: (b, i, h)),
            pl.BlockSpec((None, s, A_HEAD_DIM), lambda b, i, h: (b, 0, A_HEADS + h)),
            pl.BlockSpec((None, s, A_HEAD_DIM), lambda b, i, h: (b, 0, v_blk + h)),
            pl.BlockSpec((None, tq, IDX_HEADS * IDX_DIM), lambda b, i, h: (b, i, 0)),
            pl.BlockSpec((None, s, LANES), lambda b, i, h: (b, 0, 8)),
            pl.BlockSpec((None, tq, LANES), lambda b, i, h: (b, i, 9)),
            pl.BlockSpec((None, 3, tq, tq), lambda b, i, h: (h, 0, 0, 0)),
        ],
        out_specs=pl.BlockSpec((None, tq, A_HEAD_DIM), lambda b, i, h: (b, i, h)),
        out_shape=jax.ShapeDtypeStruct((bsz, s, A_WIDTH), BF16),
        scratch_shapes=[pltpu.VMEM((nk, tq, tq), F32), pltpu.VMEM((nk, tq, tq), jnp.int32)],
        compiler_params=_params("arbitrary", "arbitrary", "arbitrary"),
        name="dsa_attention",
    )(qkn3, qkn3, pb3, ps3, ps3, ps3, bias_tab)


def _gates_body(x_ref, al_ref, dt_ref, o_ref, *, tm):
    x = x_ref[...]
    lane = lax.broadcasted_iota(jnp.int32, (tm, LANES), 1)
    xs = x + dt_ref[...]
    sp = jnp.maximum(xs, 0.0) + jnp.log1p(jnp.exp(-jnp.abs(xs)))
    g = -jnp.exp(al_ref[...]) * sp
    r = lax.broadcasted_iota(jnp.int32, (tm, tm), 0)
    c = lax.broadcasted_iota(jnp.int32, (tm, tm), 1)
    shift = DELTA_CHUNK.bit_length() - 1
    tri = jnp.where((r >= c) & ((r >> shift) == (c >> shift)), 1.0, 0.0)
    gc = _dot(tri, g, precision=lax.Precision.HIGHEST)
    beta = jax.nn.sigmoid(x)
    o_ref[...] = jnp.where(lane < B_HEADS, gc, jnp.where(lane < 2 * B_HEADS, beta, 0.0))


def _gdn_gates(proj_small, a_log, dt_bias):
    t = proj_small.shape[0]
    tm = _pick(t, (256, 128, 64))
    pad = lambda v: jnp.pad(v.astype(F32), (0, LANES - v.shape[0])).reshape(1, LANES)
    return pl.pallas_call(
        functools.partial(_gates_body, tm=tm),
        grid=(t // tm,),
        in_specs=[pl.BlockSpec((tm, LANES), lambda i: (i, 10)),
                  pl.BlockSpec((1, LANES), lambda i: (0, 0)),
                  pl.BlockSpec((1, LANES), lambda i: (0, 0))],
        out_specs=pl.BlockSpec((tm, LANES), lambda i: (i, 0)),
        out_shape=jax.ShapeDtypeStruct((t, LANES), F32),
        compiler_params=_params("parallel"),
        name="gdn_gates",
    )(proj_small, pad(a_log), pad(dt_bias))


def _gdnprep_body(x_ref, w_ref, o_ref, xp_ref, *, s, rows):
    cb = pl.program_id(1)
    nq = B_WIDTH // (GDN_HEAD_GROUP * B_HEAD_DIM)
    width = GDN_HEAD_GROUP * B_HEAD_DIM
    xp_ref[0:8, :] = jnp.zeros((8, width), F32)
    xp_ref[8:, :] = x_ref[...].astype(F32)
    w = w_ref[...]
    is_qk = cb < 2 * nq
    qscale = jnp.where(cb < nq, B_HEAD_DIM ** -0.5, 1.0).astype(F32)

    def body(r, carry):
        r0 = pl.multiple_of(r * rows, rows)
        win = xp_ref[pl.ds(r0, rows + 8), :]
        y = w[CONV_WIDTH - 1:CONV_WIDTH] * win[8:]
        for j in range(1, CONV_WIDTH):
            y = y + w[CONV_WIDTH - 1 - j:CONV_WIDTH - j] * pltpu.roll(win, j, 0)[8:]
        y = y * jax.nn.sigmoid(y)
        for hh in range(GDN_HEAD_GROUP):
            sl = slice(hh * B_HEAD_DIM, (hh + 1) * B_HEAD_DIM)
            yh = y[:, sl]
            n = jnp.sum(yh * yh, axis=1, keepdims=True)
            yn = yh * (lax.rsqrt(n + RMS_EPS) * qscale)
            o_ref[pl.ds(r0, rows), sl] = jnp.where(is_qk, yn, yh).astype(o_ref.dtype)
        return carry

    lax.fori_loop(0, s // rows, body, 0)


def _gdn_prep(proj_big, conv_w, bsz, s):
    nb = proj_big.shape[-1]
    width = GDN_HEAD_GROUP * B_HEAD_DIM
    off = 3 * A_WIDTH // width
    ncb = 3 * B_WIDTH // width
    rows = _pick(s, (256, 128, 64))
    return pl.pallas_call(
        functools.partial(_gdnprep_body, s=s, rows=rows),
        grid=(bsz, ncb),
        in_specs=[pl.BlockSpec((None, s, width), lambda b, c: (b, 0, off + c)),
                  pl.BlockSpec((CONV_WIDTH, width), lambda b, c: (0, c))],
        out_specs=pl.BlockSpec((None, s, width), lambda b, c: (b, 0, c)),
        out_shape=jax.ShapeDtypeStruct((bsz, s, 3 * B_WIDTH), BF16),
        scratch_shapes=[pltpu.VMEM((s + 8, width), F32)],
        compiler_params=_params("parallel", "parallel"),
        name="gdn_conv_silu_norm",
    )(proj_big.reshape(bsz, s, nb), conv_w.astype(F32))


def _gdn_body(q_ref, k_ref, v_ref, z_ref, gt_ref, ng_ref, y_ref,
              u_ref, w_ref, qd_ref, kd_ref, qk_ref, egl_ref, st_ref, *, s, sup):
    hg = pl.program_id(1)
    ch = DELTA_CHUNK
    shift = ch.bit_length() - 1
    nsc = s // sup
    ncr = sup // ch
    dh = B_HEAD_DIM
    r_i = lax.broadcasted_iota(jnp.int32, (sup, sup), 0)
    c_i = lax.broadcasted_iota(jnp.int32, (sup, sup), 1)
    same = (r_i >> shift) == (c_i >> shift)
    lower = same & (r_i >= c_i)
    strict = same & (r_i > c_i)
    eye = r_i == c_i
    lastc = c_i == (r_i | (ch - 1))
    eyef = jnp.where(eye, 1.0, 0.0)
    lane = lax.broadcasted_iota(jnp.int32, (sup, LANES), 1)

    for hh in range(GDN_HEAD_GROUP):
        head = hg * GDN_HEAD_GROUP + hh
        sl = slice(hh * dh, (hh + 1) * dh)

        def p1(sc, carry, hh=hh, head=head, sl=sl):
            r0 = pl.multiple_of(sc * sup, sup)
            rows = pl.ds(r0, sup)
            q = q_ref[rows, sl].astype(F32)
            k = k_ref[rows, sl].astype(F32)
            v = v_ref[rows, sl].astype(F32)
            gt = gt_ref[rows, :]
            gc = jnp.sum(jnp.where(lane == head, gt, 0.0), axis=1, keepdims=True)
            beta = jnp.sum(jnp.where(lane == head + B_HEADS, gt, 0.0), axis=1, keepdims=True)
            grow = jnp.sum(jnp.where(eye, gc, 0.0), axis=0, keepdims=True)
            glast = jnp.sum(jnp.where(lastc, grow, 0.0), axis=1, keepdims=True)
            decay = jnp.where(lower, jnp.exp(jnp.where(lower, gc - grow, 0.0)), 0.0)
            kb = k * beta
            kbf = k.astype(BF16)
            a_mat = jnp.where(strict, _dot_nt(kb.astype(BF16), kbf) * decay, 0.0)
            p = eyef - a_mat
            x = a_mat.astype(BF16)
            for _ in range(shift - 1):
                x2 = _dot(x, x)
                p = _dot(p.astype(BF16), (eyef + x2).astype(BF16))
                x = x2.astype(BF16)
            tb = p.astype(BF16)
            eg = jnp.exp(gc)
            u_ref[hh, rows, :] = _dot(tb, (v * beta).astype(BF16))
            w_ref[hh, rows, :] = _dot(tb, (kb * eg).astype(BF16)).astype(BF16)
            qk = jnp.where(lower, _dot_nt(q.astype(BF16), kbf) * decay, 0.0)
            qkc = qk[:, 0:ch]
            for cc in range(1, ncr):
                qkc = qkc + qk[:, cc * ch:(cc + 1) * ch]
            qk_ref[hh, rows, :] = qkc.astype(BF16)
            qd_ref[hh, rows, :] = (q * eg).astype(BF16)
            kd_ref[hh, rows, :] = (k * jnp.exp(glast - gc)).astype(BF16)
            for cc in range(ncr):
                e = jnp.exp(glast[cc * ch:cc * ch + 8, :])
                e0 = pl.multiple_of((sc * ncr + cc) * 8, 8)
                egl_ref[hh, pl.ds(e0, 8), :] = jnp.broadcast_to(e, (8, LANES))
            return carry

        lax.fori_loop(0, nsc, p1, 0)

    st_ref[...] = jnp.zeros(st_ref.shape, F32)
    ng = ng_ref[...]

    def p2(c, carry):
        r0 = pl.multiple_of(c * ch, ch)
        rows = pl.ds(r0, ch)
        for hh in range(GDN_HEAD_GROUP):
            sl = slice(hh * dh, (hh + 1) * dh)
            sf = st_ref[hh]
            sb = sf.astype(BF16)
            vnew = u_ref[hh, rows, :] - _dot(w_ref[hh, rows, :], sb)
            vb16 = vnew.astype(BF16)
            o = _dot(qd_ref[hh, rows, :], sb) + _dot(qk_ref[hh, rows, :], vb16)
            egl = egl_ref[hh, pl.ds(pl.multiple_of(c * 8, 8), 8), :]
            st_ref[hh] = sf * jnp.tile(egl, (dh // 8, 1)) + _dot_tn(kd_ref[hh, rows, :], vb16)
            ms = jnp.mean(o * o, axis=1, keepdims=True)
            zz = z_ref[rows, sl].astype(F32)
            y = o * lax.rsqrt(ms + RMS_EPS) * ng * (zz * jax.nn.sigmoid(zz))
            y_ref[rows, sl] = y.astype(y_ref.dtype)
        return carry

    lax.fori_loop(0, s // ch, p2, 0)


def _gdn(qkv, proj_big, gates, gdn_norm_g, bsz, s):
    nb = proj_big.shape[-1]
    width = GDN_HEAD_GROUP * B_HEAD_DIM
    ngrp = B_HEADS // GDN_HEAD_GROUP
    sup = _pick(s, (GDN_SUPER, 128, 64))
    z_off = (3 * A_WIDTH + 3 * B_WIDTH) // width
    hgp = GDN_HEAD_GROUP
    return pl.pallas_call(
        functools.partial(_gdn_body, s=s, sup=sup),
        grid=(bsz, ngrp),
        in_specs=[
            pl.BlockSpec((None, s, width), lambda b, g: (b, 0, g)),
            pl.BlockSpec((None, s, width), lambda b, g: (b, 0, ngrp + g)),
            pl.BlockSpec((None, s, width), lambda b, g: (b, 0, 2 * ngrp + g)),
            pl.BlockSpec((None, s, width), lambda b, g: (b, 0, z_off + g)),
            pl.BlockSpec((None, s, LANES), lambda b, g: (b, 0, 0)),
            pl.BlockSpec((1, B_HEAD_DIM), lambda b, g: (0, 0)),
        ],
        out_specs=pl.BlockSpec((None, s, width), lambda b, g: (b, 0, g)),
        out_shape=jax.ShapeDtypeStruct((bsz, s, B_WIDTH), BF16),
        scratch_shapes=[
            pltpu.VMEM((hgp, s, B_HEAD_DIM), F32),
            pltpu.VMEM((hgp, s, B_HEAD_DIM), BF16),
            pltpu.VMEM((hgp, s, B_HEAD_DIM), BF16),
            pltpu.VMEM((hgp, s, B_HEAD_DIM), BF16),
            pltpu.VMEM((hgp, s, DELTA_CHUNK), BF16),
            pltpu.VMEM((hgp, s // DELTA_CHUNK * 8, LANES), F32),
            pltpu.VMEM((hgp, B_HEAD_DIM, B_HEAD_DIM), F32),
        ],
        compiler_params=_params("parallel", "arbitrary"),
        name="gdn_delta_rule",
    )(qkv, qkv, qkv, proj_big.reshape(bsz, s, nb), gates.reshape(bsz, s, LANES),
      gdn_norm_g.reshape(1, B_HEAD_DIM).astype(F32))


def _merge_body(ya_ref, wa_ref, yb_ref, wb_ref, ga_ref, gb_ref, o_ref):
    a = _dot(ya_ref[...], wa_ref[...])
    b = _dot(yb_ref[...], wb_ref[...])
    o = jax.nn.sigmoid(ga_ref[...].astype(F32)) * a + jax.nn.sigmoid(gb_ref[...].astype(F32)) * b
    o_ref[...] = o.astype(o_ref.dtype)


def _merge(y_a, y_b, wa, wb, proj_big, d):
    t = y_a.shape[0]
    bm = _pick(t, (512, 256, 128))
    bn = _pick(d, (1024, 512, 256, 128))
    ga_blk = (3 * A_WIDTH + 4 * B_WIDTH) // bn
    gb_blk = ga_blk + d // bn
    return pl.pallas_call(
        _merge_body,
        grid=(t // bm, d // bn),
        in_specs=[
            pl.BlockSpec((bm, A_WIDTH), lambda i, j: (i, 0)),
            pl.BlockSpec((A_WIDTH, bn), lambda i, j: (0, j)),
            pl.BlockSpec((bm, B_WIDTH), lambda i, j: (i, 0)),
            pl.BlockSpec((B_WIDTH, bn), lambda i, j: (0, j)),
            pl.BlockSpec((bm, bn), lambda i, j: (i, ga_blk + j)),
            pl.BlockSpec((bm, bn), lambda i, j: (i, gb_blk + j)),
        ],
        out_specs=pl.BlockSpec((bm, bn), lambda i, j: (i, j)),
        out_shape=jax.ShapeDtypeStruct((t, d), BF16),
        compiler_params=_params("parallel", "parallel"),
        name="gated_merge",
    )(y_a, wa, y_b, wb, proj_big, proj_big)


def _peerprep_body(q_ref, k1_ref, k2_ref, s1_ref, s2_ref, sc_ref, *, tm):
    nkeys = PEER_N_KEYS
    half = PEER_QUERY_DIM // 2
    s1 = _dot_nt(k1_ref[...].astype(BF16), q_ref[:, 0:half].astype(BF16))
    s2 = _dot_nt(k2_ref[...].astype(BF16), q_ref[:, half:2 * half].astype(BF16))
    out_row = lax.broadcasted_iota(jnp.int32, (PEER_TOPK, tm), 0)

    def top(sv):
        n = sv.shape[0]
        row = lax.broadcasted_iota(jnp.int32, (n, tm), 0).astype(F32)

        def it(r, carry):
            work, vals = carry
            m = jnp.max(work, axis=0, keepdims=True)
            idx = jnp.min(jnp.where(work == m, row, float(n)), axis=0, keepdims=True)
            work = jnp.where(row == idx, -jnp.inf, work)
            vals = jnp.where(out_row == r, m, vals)
            return work, vals

        return lax.fori_loop(0, PEER_TOPK, it, (sv, jnp.zeros((PEER_TOPK, tm), F32)))

    w1, v1 = top(s1)
    w2, v2 = top(s2)
    cand = jnp.concatenate([v1[a:a + 1, :] + v2 for a in range(PEER_TOPK)], axis=0)
    _, vc = top(cand)
    c0 = vc[0:1, :]
    z = jnp.sum(jnp.exp(vc - c0), axis=0, keepdims=True)
    s1_ref[...] = jnp.where(w1 == -jnp.inf, s1, -jnp.inf)
    s2_ref[...] = jnp.where(w2 == -jnp.inf, s2, -jnp.inf)
    orow = lax.broadcasted_iota(jnp.int32, (8, tm), 0)
    sc_ref[...] = jnp.where(orow == 0, vc[PEER_TOPK - 1:PEER_TOPK, :],
                            jnp.where(orow == 1, c0 + jnp.log(z), 0.0))


def _peer_prep(qp, k1, k2):
    t = qp.shape[0]
    tm = _pick(t, (256, 128))
    nkeys = PEER_N_KEYS
    half = PEER_QUERY_DIM // 2
    return pl.pallas_call(
        functools.partial(_peerprep_body, tm=tm),
        grid=(t // tm, PEER_HEADS),
        in_specs=[pl.BlockSpec((tm, PEER_QUERY_DIM), lambda i, h: (i, h)),
                  pl.BlockSpec((nkeys, half), lambda i, h: (0, 0)),
                  pl.BlockSpec((nkeys, half), lambda i, h: (0, 0))],
        out_specs=[pl.BlockSpec((None, nkeys, tm), lambda i, h: (h, 0, i)),
                   pl.BlockSpec((None, nkeys, tm), lambda i, h: (h, 0, i)),
                   pl.BlockSpec((None, 8, tm), lambda i, h: (h, 0, i))],
        out_shape=[jax.ShapeDtypeStruct((PEER_HEADS, nkeys, t), F32),
                   jax.ShapeDtypeStruct((PEER_HEADS, nkeys, t), F32),
                   jax.ShapeDtypeStruct((PEER_HEADS, 8, t), F32)],
        compiler_params=_params("parallel", "parallel"),
        name="peer_topk",
    )(qp, k1.astype(F32), k2.astype(F32))


def _peer_act_body(x_ref, u_ref, s1_ref, s2_ref, thr_ref, cz_ref, h_ref, acc_ref, *, tm, te, rb):
    nkeys = PEER_N_KEYS
    acc_ref[...] = _dot_nt(x_ref[...], u_ref[...])
    inv_sqrt2 = 1.0 / math.sqrt(2.0)

    def body(r, carry):
        rows = pl.ds(pl.multiple_of(r * rb, rb), rb)
        s1 = s1_ref[rows, :]
        for ii in range(te // nkeys):
            g = jnp.zeros((rb, nkeys), F32)
            for h in range(PEER_HEADS):
                hs = slice(h * nkeys, (h + 1) * nkeys)
                lane = ii * PEER_HEADS + h
                sm = s1[:, lane:lane + 1] + s2_ref[rows, hs]
                g = g + jnp.where(sm >= thr_ref[rows, hs], jnp.exp(sm - cz_ref[rows, hs]), 0.0)
            es = slice(ii * nkeys, (ii + 1) * nkeys)
            a = acc_ref[rows, es]
            act = 0.5 * a * (1.0 + lax.erf(a * inv_sqrt2))
            h_ref[rows, es] = (g * act).astype(h_ref.dtype)
        return carry

    lax.fori_loop(0, tm // rb, body, 0)


def _peer_layout(s1t, s2t, sct):
    t = s1t.shape[-1]
    hw = PEER_HEADS * PEER_N_KEYS
    used = PEER_ACT_FIRST_KEYS * PEER_HEADS
    s1 = s1t.transpose(2, 1, 0).reshape(t, PEER_N_KEYS // PEER_ACT_FIRST_KEYS, used)
    s1 = jnp.pad(s1, ((0, 0), (0, 0), (0, LANES - used))).reshape(t, -1)
    s2 = s2t.transpose(2, 0, 1).reshape(t, hw)
    thrb = jnp.repeat(sct[:, 0, :].T, PEER_N_KEYS, axis=1)
    czb = jnp.repeat(sct[:, 1, :].T, PEER_N_KEYS, axis=1)
    return s1, s2, thrb, czb


def _peer_act(xn2, u_bf, s1, s2, thrb, czb):
    t, d = xn2.shape
    e = u_bf.shape[0]
    nkeys = PEER_N_KEYS
    te = PEER_ACT_FIRST_KEYS * nkeys
    tm = _pick(t, (512, 256, 128))
    rb = _pick(tm, (64, 32, 16, 8))
    hw = PEER_HEADS * nkeys
    return pl.pallas_call(
        functools.partial(_peer_act_body, tm=tm, te=te, rb=rb),
        grid=(t // tm, e // te),
        in_specs=[pl.BlockSpec((tm, d), lambda i, j: (i, 0)),
                  pl.BlockSpec((te, d), lambda i, j: (j, 0)),
                  pl.BlockSpec((tm, LANES), lambda i, j: (i, j)),
                  pl.BlockSpec((tm, hw), lambda i, j: (i, 0)),
                  pl.BlockSpec((tm, hw), lambda i, j: (i, 0)),
                  pl.BlockSpec((tm, hw), lambda i, j: (i, 0))],
        out_specs=pl.BlockSpec((tm, te), lambda i, j: (i, j)),
        out_shape=jax.ShapeDtypeStruct((t, e), BF16),
        scratch_shapes=[pltpu.VMEM((tm, te), F32)],
        compiler_params=_params("parallel", "arbitrary"),
        name="peer_gate_act",
    )(xn2, u_bf, s1, s2, thrb, czb)


def _split_offsets(d):
    sizes = (A_WIDTH, A_WIDTH, A_WIDTH, IDX_HEADS * IDX_DIM, IDX_DIM, IDX_HEADS,
             3 * B_WIDTH, B_HEADS, B_HEADS, B_WIDTH, d, d)
    offs = [0]
    for sz in sizes:
        offs.append(offs[-1] + sz)
    return offs


def _layer(x, norm1_g, w_in, conv_w, a_log, dt_bias, gdn_norm_g, q_norm_g, k_norm_g, rel_bias,
           w_br_a, w_br_b, w_out, norm2_g, peer_wq, peer_k1, peer_k2, peer_u, peer_v):
    bsz, s, d = x.shape
    t = bsz * s
    x2d = x.reshape(t, d)
    o = _split_offsets(d)
    col = lambda a, b: w_in[:, o[a]:o[b]]
    padl = lambda w: jnp.pad(w, ((0, 0), (0, LANES - w.shape[1])))
    w_big = jnp.concatenate([col(0, 3), col(6, 7), col(9, 12)], axis=1).astype(BF16)
    w_small = jnp.concatenate([col(3, 4), padl(col(4, 5)), padl(col(5, 6)), padl(col(7, 9))], axis=1).astype(BF16)

    xn = _rmsnorm(x2d, norm1_g)
    bm = _pick(t, (1024, 512, 256, 128))
    nb = w_big.shape[1]
    proj_big = _matmul(xn, w_big, out_dtype=BF16, bm=bm, bn=_pick(nb, (1024, 512, 256, 128)), name="in_proj_big")
    proj_small = _matmul(xn, w_small, out_dtype=F32, bm=bm, bn=w_small.shape[1], name="in_proj_small")

    tq = _pick(s, (256, 128))
    topk = min(IDX_TOPK_MAX, s // 4)
    assert tq >= topk and tq >= REL_MAX_DIST
    g_row = jnp.concatenate([jnp.tile(q_norm_g.astype(F32) * (A_HEAD_DIM ** -0.5), A_HEADS),
                             jnp.tile(k_norm_g.astype(F32), A_HEADS)]).reshape(1, 2 * A_WIDTH)
    qkn = _qknorm(proj_big, g_row)
    bias_tab = _bias_table(rel_bias, tq)
    y_a = _dsa(qkn, proj_big, proj_small, bias_tab, bsz, s, tq, topk).reshape(t, A_WIDTH)

    gates = _gdn_gates(proj_small, a_log, dt_bias)
    qkv = _gdn_prep(proj_big, conv_w, bsz, s)
    y_b = _gdn(qkv, proj_big, gates, gdn_norm_g, bsz, s).reshape(t, B_WIDTH)

    merged = _merge(y_a, y_b, w_br_a.astype(BF16), w_br_b.astype(BF16), proj_big, d)
    bn_d = _pick(d, (1024, 512, 256, 128))
    x2 = _matmul(merged, w_out.astype(BF16), out_dtype=F32, bm=_pick(t, (512, 256, 128)), bn=bn_d,
                 residual=x2d, name="out_proj")

    xn2 = _rmsnorm(x2, norm2_g)
    qp = _matmul(xn2, peer_wq.astype(BF16), out_dtype=F32, bm=bm, bn=_pick(peer_wq.shape[1], (1024, 512, 256)),
                 name="peer_query")
    s1t, s2t, sct = _peer_prep(qp, peer_k1, peer_k2)
    hmat = _peer_act(xn2, peer_u.astype(BF16), *_peer_layout(s1t, s2t, sct))
    e = peer_u.shape[0]
    out = _matmul(hmat, peer_v.astype(BF16), out_dtype=F32, bm=bm, bn=bn_d, bk=_pick(e, (2048, 1024, 512)),
                  residual=x2, name="peer_out")
    return out.reshape(bsz, s, d)


def kernel(x, norm1_g, w_in, conv_w, a_log, dt_bias, gdn_norm_g, q_norm_g, k_norm_g, rel_bias, w_br_a, w_br_b, w_out, norm2_g, peer_wq, peer_k1, peer_k2, peer_u, peer_v):
    for l in range(norm1_g.shape[0]):
        x = _layer(x, norm1_g[l], w_in[l], conv_w[l], a_log[l], dt_bias[l], gdn_norm_g[l], q_norm_g[l],
                   k_norm_g[l], rel_bias, w_br_a[l], w_br_b[l], w_out[l], norm2_g[l], peer_wq[l],
                   peer_k1[l], peer_k2[l], peer_u[l], peer_v[l])
    return x
```
